```python
import math
import jax
import jax.numpy as jnp
from jax import lax
import numpy as np

D_MODEL = 4096
BATCH = 4
SEQ = 2048
DEPTH = 4
DEC_BATCH = 32
DEC_SEQ = 1
PAST_LEN = 8192
PAGE_SIZE = 128

N_A_LAYERS = DEPTH // 2
N_B_LAYERS = DEPTH - N_A_LAYERS
HEAD_DIM = 128
N_HEADS = D_MODEL // HEAD_DIM
N_KV_A = 4
N_KV_B = 8
L_CMP = 32
CMP_STRIDE = 16
CMP_HID = 2 * HEAD_DIM
SEL_BLOCK = 64
N_SEL = 16
WIN_A = 512
WIN_B = 128
NUM_BUCKETS = 32
MAX_DISTANCE = 2048
D_FF = 11008
CONV_W = 3
RMS_EPS = 1e-6
Q_BLOCK = 128
SEL_Q_BLOCK = 16
IN_A = N_HEADS * HEAD_DIM + 6 * N_KV_A * HEAD_DIM + 3 * N_HEADS

kernel_name = 'yoco_nsa_swa_sink_convffn_step'


def rms_norm(x, g):
    xf = x.astype(jnp.float32)
    y = xf * lax.rsqrt(jnp.mean(xf * xf, axis=-1, keepdims=True) + RMS_EPS)
    return (y * g.astype(jnp.float32)).astype(x.dtype)


def rel_bucket(rel):
    n = jnp.maximum(rel, 0)
    max_exact = NUM_BUCKETS // 2
    nf = jnp.maximum(n, 1).astype(jnp.float32)
    large = max_exact + (jnp.log(nf / max_exact) / math.log(MAX_DISTANCE / max_exact)
                         * (NUM_BUCKETS - max_exact)).astype(jnp.int32)
    large = jnp.minimum(large, NUM_BUCKETS - 1)
    return jnp.where(n < max_exact, n, large)


def head_bias(rel, rel_bias, n_groups):
    tq, s = rel.shape
    b = rel_bias.astype(jnp.float32)[rel_bucket(rel)]
    return b.reshape(tq, s, n_groups, N_HEADS // n_groups).transpose(2, 3, 0, 1)


def masked_softmax(logits, mask, sink=None):
    neg = jnp.finfo(jnp.float32).min
    l = jnp.where(mask, logits, neg)
    m = jnp.max(l, axis=-1, keepdims=True)
    if sink is not None:
        m = jnp.maximum(m, sink)
    e = jnp.where(mask, jnp.exp(l - m), 0.0)
    den = jnp.sum(e, axis=-1, keepdims=True)
    if sink is not None:
        den = den + jnp.exp(sink - m)
    return e / jnp.maximum(den, 1e-30)


def attend_shared(q, k, v, qpos, kpos, window, rel_bias, sink=None):
    b, tq, h, dh = q.shape
    g = k.shape[2]
    qg = q.reshape(b, tq, g, h // g, dh)
    s = jnp.einsum('btgrd,bsgd->bgrts', qg, k).astype(jnp.float32) * (dh ** -0.5)
    rel = qpos[:, None] - kpos[None, :]
    mask = (rel >= 0) & (rel <= window) & (kpos >= 0)[None, :]
    s = s + head_bias(rel, rel_bias, g)
    sk = None if sink is None else sink.astype(jnp.float32).reshape(g, h // g, 1, 1)
    p = masked_softmax(s, mask, sk)
    o = jnp.einsum('bgrts,bsgd->btgrd', p.astype(v.dtype), v)
    return o.reshape(b, tq, h, dh)


def window_attn_prompt(q, k, v, window, rel_bias, sink=None):
    b, t, h, dh = q.shape
    nq = t // Q_BLOCK
    span = Q_BLOCK + window
    kp = jnp.pad(k, ((0, 0), (window, 0), (0, 0), (0, 0)))
    vp = jnp.pad(v, ((0, 0), (window, 0), (0, 0), (0, 0)))
    qb = jnp.moveaxis(q.reshape(b, nq, Q_BLOCK, h, dh), 1, 0)

    def blk(args):
        i, qi = args
        start = i * Q_BLOCK
        ki = lax.dynamic_slice_in_dim(kp, start, span, axis=1)
        vi = lax.dynamic_slice_in_dim(vp, start, span, axis=1)
        qpos = start + jnp.arange(Q_BLOCK)
        kpos = start - window + jnp.arange(span)
        return attend_shared(qi, ki, vi, qpos, kpos, window, rel_bias, sink)

    out = lax.map(blk, (jnp.arange(nq), qb))
    return jnp.moveaxis(out, 0, 1).reshape(b, t, h, dh)


def half_block_proj(rows, w1):
    b, l, c, g, dh = rows.shape
    nh = l // CMP_STRIDE
    half = rows[:, :nh * CMP_STRIDE].reshape(b, nh, CMP_STRIDE, c, g, dh)
    w1r = w1.reshape(2, 2, CMP_STRIDE, dh, CMP_HID)
    return jnp.einsum('bnscgd,cesdh->ebncgh', half, w1r)


def compress_from_halves(hb, w1, w2, pe):
    pe_term = jnp.einsum('cld,cldh->ch', pe, w1.reshape(2, L_CMP, HEAD_DIM, CMP_HID))
    h = jax.nn.gelu(hb[0, :, :-1] + hb[1, :, 1:] + pe_term[:, None, :])
    return jnp.einsum('bncgh,chd->bncgd', h, w2)


def nsa_cmp_sel(q, qpos, kc, vc, nsel, fetch, rel_bias):
    b, tq, h, dh = q.shape
    g = kc.shape[2]
    r = h // g
    nc = kc.shape[1]
    scale = dh ** -0.5
    qg = q.reshape(b, tq, g, r, dh)
    cend = CMP_STRIDE * jnp.arange(nc) + (L_CMP - 1)
    rel_c = qpos[:, None] - cend[None, :]
    s_c = jnp.einsum('btgrd,bcgd->bgrtc', qg, kc).astype(jnp.float32) * scale + head_bias(rel_c, rel_bias, g)
    p_c = masked_softmax(s_c, rel_c >= 0)
    o_c = jnp.einsum('bgrtc,bcgd->btgrd', p_c.astype(vc.dtype), vc).reshape(b, tq, h, dh)
    cst = CMP_STRIDE * jnp.arange(nc)[:, None]
    sst = SEL_BLOCK * jnp.arange(nsel)[None, :]
    ov = jnp.maximum(jnp.minimum(cst + L_CMP, sst + SEL_BLOCK) - jnp.maximum(cst, sst), 0).astype(jnp.float32) / L_CMP
    imp = jnp.einsum('bgrtc,cj->bgtj', p_c, ov)
    jblk = jnp.arange(nsel)[None, :]
    cur = (qpos // SEL_BLOCK)[:, None]
    forced = (jblk == 0) | (jblk == cur) | (jblk == cur - 1)
    imp = jnp.where(forced, jnp.inf, jnp.where(jblk <= cur, imp, -jnp.inf))
    k_top = min(N_SEL, nsel)
    _, idx = lax.top_k(imp, k_top)
    kv = fetch(idx)
    s_len = k_top * SEL_BLOCK
    ks = kv[..., 0, :].reshape(b, g, tq, s_len, dh)
    vs = kv[..., 1, :].reshape(b, g, tq, s_len, dh)
    kpos = (idx[..., None] * SEL_BLOCK + jnp.arange(SEL_BLOCK)).reshape(b, g, tq, s_len)
    rel_s = qpos[None, None, :, None] - kpos
    tab = rel_bias.astype(jnp.float32).reshape(NUM_BUCKETS, g, r)
    bias_s = tab[rel_bucket(rel_s), jnp.arange(g)[None, :, None, None]]
    s_s = jnp.einsum('btgrd,bgtsd->bgrts', qg, ks).astype(jnp.float32) * scale + jnp.moveaxis(bias_s, -1, 2)
    p_s = masked_softmax(s_s, (rel_s >= 0)[:, :, None])
    o_s = jnp.einsum('bgrts,bgtsd->btgrd', p_s.astype(vs.dtype), vs).reshape(b, tq, h, dh)
    return o_c, o_s


def nsa_project(h, w_in):
    b, t, _ = h.shape
    z = h @ w_in
    nq = N_HEADS * HEAD_DIM
    nkv = 6 * N_KV_A * HEAD_DIM
    q = z[..., :nq].reshape(b, t, N_HEADS, HEAD_DIM)
    kv = z[..., nq:nq + nkv].reshape(b, t, 3, 2, N_KV_A, HEAD_DIM)
    gates = jax.nn.sigmoid(z[..., nq + nkv:].astype(jnp.float32)).astype(h.dtype).reshape(b, t, N_HEADS, 3)
    return q, kv, gates


def nsa_merge(o_c, o_s, o_w, gates, w_out):
    o = o_c * gates[..., 0:1] + o_s * gates[..., 1:2] + o_w * gates[..., 2:3]
    b, t = o.shape[:2]
    return o.reshape(b, t, N_HEADS * HEAD_DIM) @ w_out


def nsa_prompt(h, rel_bias, w_in, w1, w2, pe, w_out):
    q, kv, gates = nsa_project(h, w_in)
    b, t = h.shape[:2]
    comp = compress_from_halves(half_block_proj(kv[:, :, 0], w1), w1, w2, pe)
    kc, vc = comp[:, :, 0], comp[:, :, 1]
    blocks = kv[:, :, 1].reshape(b, t // SEL_BLOCK, SEL_BLOCK, 2, N_KV_A, HEAD_DIM)
    b6 = jnp.arange(b)[:, None, None, None, None, None]
    g6 = jnp.arange(N_KV_A)[None, :, None, None, None, None]
    sb = jnp.arange(SEL_BLOCK)[:, None]
    kvi = jnp.arange(2)

    def fetch(idx):
        return blocks[b6, idx[..., None, None], sb, kvi, g6]

    nsel = t // SEL_BLOCK
    nqb = t // SEL_Q_BLOCK
    qb = jnp.moveaxis(q.reshape(b, nqb, SEL_Q_BLOCK, N_HEADS, HEAD_DIM), 1, 0)

    def blk(args):
        i, qi = args
        qpos = i * SEL_Q_BLOCK + jnp.arange(SEL_Q_BLOCK)
        return nsa_cmp_sel(qi, qpos, kc, vc, nsel, fetch, rel_bias)

    o_c, o_s = lax.map(blk, (jnp.arange(nqb), qb))
    o_c = jnp.moveaxis(o_c, 0, 1).reshape(b, t, N_HEADS, HEAD_DIM)
    o_s = jnp.moveaxis(o_s, 0, 1).reshape(b, t, N_HEADS, HEAD_DIM)
    o_w = window_attn_prompt(q, kv[:, :, 2, 0], kv[:, :, 2, 1], WIN_A, rel_bias)
    out = nsa_merge(o_c, o_s, o_w, gates, w_out)
    rows = kv[:, :, :2].reshape(b, t, 4, N_KV_A, HEAD_DIM)
    buf = kv[:, t - min(WIN_A, t):, 2]
    return out, rows, buf


def nsa_sample(h, layer, cache_kv_a, win_buf, page_table, rel_bias, w_in, w1, w2, pe, w_out):
    q, kv, gates = nsa_project(h, w_in)
    b, t = h.shape[:2]
    n_pages = page_table.shape[1]
    past = n_pages * PAGE_SIZE
    past_cmp = cache_kv_a[page_table[:, :, None, None], jnp.arange(PAGE_SIZE)[None, None, :, None],
                          layer, jnp.arange(2)[None, None, None, :]]
    past_cmp = past_cmp.reshape(b, past, 2, N_KV_A, HEAD_DIM)
    hb = jnp.concatenate([half_block_proj(past_cmp, w1), half_block_proj(kv[:, :, 0], w1)], axis=2)
    comp = compress_from_halves(hb, w1, w2, pe)
    kc, vc = comp[:, :, 0], comp[:, :, 1]
    npb = past // SEL_BLOCK
    r_pg = PAGE_SIZE // SEL_BLOCK
    pool_r = cache_kv_a.reshape(cache_kv_a.shape[0], r_pg, SEL_BLOCK, N_A_LAYERS, 4, N_KV_A, HEAD_DIM)
    n_tail = -(-t // SEL_BLOCK)
    tail = jnp.pad(kv[:, :, 1], ((0, 0), (0, n_tail * SEL_BLOCK - t), (0, 0), (0, 0), (0, 0)))
    tail = tail.reshape(b, n_tail, SEL_BLOCK, 2, N_KV_A, HEAD_DIM)
    b6 = jnp.arange(b)[:, None, None, None, None, None]
    g6 = jnp.arange(N_KV_A)[None, :, None, None, None, None]
    sb = jnp.arange(SEL_BLOCK)[:, None]
    kvi = jnp.arange(2)

    def fetch(idx):
        i6 = idx[..., None, None]
        jp = jnp.minimum(i6, npb - 1)
        phys = page_table[b6, jp // r_pg]
        from_pool = pool_r[phys, jp % r_pg, sb, layer, kvi + 2, g6]
        jt = jnp.clip(i6 - npb, 0, n_tail - 1)
        from_tail = tail[b6, jt, sb, kvi, g6]
        return jnp.where(i6[..., None] >= npb, from_tail, from_pool)

    nsel = -(-(past + t) // SEL_BLOCK)
    qpos = past + jnp.arange(t)
    o_c, o_s = nsa_cmp_sel(q, qpos, kc, vc, nsel, fetch, rel_bias)
    wb = win_buf.shape[1]
    ext = jnp.concatenate([win_buf.astype(kv.dtype), kv[:, :, 2]], axis=1)
    kpos = past - wb + jnp.arange(wb + t)
    o_w = attend_shared(q, ext[:, :, 0], ext[:, :, 1], qpos, kpos, WIN_A, rel_bias)
    out = nsa_merge(o_c, o_s, o_w, gates, w_out)
    rows = kv[:, :, :2].reshape(b, t, 4, N_KV_A, HEAD_DIM)
    return out, rows, ext[:, t:]


def shared_kv(x, g, w_kv):
    b, t, _ = x.shape
    return (rms_norm(x, g) @ w_kv).reshape(b, t, 2, N_KV_B, HEAD_DIM)


def conv_ffn(h, prev, w_up, w_conv, b_conv, w_down):
    u = h @ w_up
    t = u.shape[1]
    ext = jnp.concatenate([prev.astype(u.dtype), u], axis=1)
    c = b_conv + w_conv[0] * ext[:, 0:t]
    for k in range(1, CONV_W):
        c = c + w_conv[k] * ext[:, k:k + t]
    a, v = jnp.split(c, 2, axis=-1)
    return (jax.nn.silu(a) * v) @ w_down, ext[:, t:]


def setup_inputs(seed: int = 0) -> dict:
    key = jax.random.key(seed)
    ks = jax.random.split(key, 32)
    f32 = jnp.float32

    def nrm(i, shape, scale):
        return jax.random.normal(ks[i], shape, f32) * scale

    n_pages = PAST_LEN // PAGE_SIZE
    n_phys = (5 * DEC_BATCH * n_pages) // 4
    wb_a = min(WIN_A, PAST_LEN)
    wb_b = min(WIN_B, PAST_LEN)
    perm = jax.random.permutation(ks[6], n_phys)
    page_table = perm[:DEC_BATCH * n_pages].reshape(DEC_BATCH, n_pages).astype(jnp.int32)
    qd = N_HEADS * HEAD_DIM
    return {
        'x_prompt': nrm(0, (BATCH, SEQ, D_MODEL), 1.0),
        'x_sample': nrm(1, (DEC_BATCH, DEC_SEQ, D_MODEL), 1.0),
        'cache_kv_a': nrm(2, (n_phys, PAGE_SIZE, N_A_LAYERS, 4, N_KV_A, HEAD_DIM), 1.0),
        'state_win_a': nrm(3, (N_A_LAYERS, DEC_BATCH, wb_a, 2, N_KV_A, HEAD_DIM), 1.0),
        'state_kv_b': nrm(4, (DEC_BATCH, wb_b, 2, N_KV_B, HEAD_DIM), 1.0),
        'state_conv': nrm(5, (DEPTH, DEC_BATCH, CONV_W - 1, 2 * D_FF), 1.0),
        'page_table': page_table,
        'rel_bias': nrm(7, (NUM_BUCKETS, N_HEADS), 0.5),
        'norm_attn': 1.0 + nrm(8, (DEPTH, D_MODEL), 0.02),
        'norm_ffn': 1.0 + nrm(9, (DEPTH, D_MODEL), 0.02),
        'norm_kv_b': 1.0 + nrm(10, (D_MODEL,), 0.02),
        'norm_final': 1.0 + nrm(11, (D_MODEL,), 0.02),
        'w_in_a': nrm(12, (N_A_LAYERS, D_MODEL, IN_A), D_MODEL ** -0.5),
        'w_cmp1': nrm(13, (N_A_LAYERS, 2, L_CMP * HEAD_DIM, CMP_HID), (L_CMP * HEAD_DIM) ** -0.5),
        'w_cmp2': nrm(14, (N_A_LAYERS, 2, CMP_HID, HEAD_DIM), CMP_HID ** -0.5),
        'pe_cmp': nrm(15, (N_A_LAYERS, 2, L_CMP, HEAD_DIM), 0.1),
        'w_out_a': nrm(16, (N_A_LAYERS, qd, D_MODEL), qd ** -0.5),
        'w_kv_b': nrm(17, (D_MODEL, 2 * N_KV_B * HEAD_DIM), D_MODEL ** -0.5),
        'w_q_b': nrm(18, (N_B_LAYERS, D_MODEL, qd), D_MODEL ** -0.5),
        'sinks_b': nrm(19, (N_B_LAYERS, N_HEADS), 0.5),
        'w_out_b': nrm(20, (N_B_LAYERS, qd, D_MODEL), qd ** -0.5),
        'w_up': nrm(21, (DEPTH, D_MODEL, 2 * D_FF), D_MODEL ** -0.5),
        'w_conv': nrm(22, (DEPTH, CONV_W, 2 * D_FF), CONV_W ** -0.5),
        'b_conv': nrm(23, (DEPTH, 2 * D_FF), 0.01),
        'w_down': nrm(24, (DEPTH, D_FF, D_MODEL), D_FF ** -0.5),
    }


def reference(x_prompt, x_sample, cache_kv_a, state_win_a, state_kv_b, state_conv, page_table,
              rel_bias, norm_attn, norm_ffn, norm_kv_b, norm_final, w_in_a, w_cmp1, w_cmp2, pe_cmp,
              w_out_a, w_kv_b, w_q_b, sinks_b, w_out_b, w_up, w_conv, b_conv, w_down):
    xp, xs = x_prompt, x_sample
    t_p = xp.shape[1]
    t_s = xs.shape[1]
    past_len = page_table.shape[1] * PAGE_SIZE
    qpos_s = past_len + jnp.arange(t_s)
    wb_b = state_kv_b.shape[1]
    rows_p, rows_s, win_p, win_s, conv_p, conv_s = [], [], [], [], [], []
    for layer in range(DEPTH):
        hp = rms_norm(xp, norm_attn[layer])
        hs = rms_norm(xs, norm_attn[layer])
        if layer < N_A_LAYERS:
            o_p, r_p, b_p = nsa_prompt(hp, rel_bias, w_in_a[layer], w_cmp1[layer], w_cmp2[layer],
                                       pe_cmp[layer], w_out_a[layer])
            o_s, r_s, b_s = nsa_sample(hs, layer, cache_kv_a, state_win_a[layer], page_table, rel_bias,
                                       w_in_a[layer], w_cmp1[layer], w_cmp2[layer], pe_cmp[layer], w_out_a[layer])
            rows_p.append(r_p)
            rows_s.append(r_s)
            win_p.append(b_p)
            win_s.append(b_s)
        else:
            if layer == N_A_LAYERS:
                kvb_p = shared_kv(xp, norm_kv_b, w_kv_b)
                ext_b = jnp.concatenate([state_kv_b.astype(xs.dtype), shared_kv(xs, norm_kv_b, w_kv_b)], axis=1)
                kpos_b = past_len - wb_b + jnp.arange(ext_b.shape[1])
            lb = layer - N_A_LAYERS
            bp, tp = hp.shape[:2]
            q_p = (hp @ w_q_b[lb]).reshape(bp, tp, N_HEADS, HEAD_DIM)
            a_p = window_attn_prompt(q_p, kvb_p[:, :, 0], kvb_p[:, :, 1], WIN_B, rel_bias, sinks_b[lb])
            o_p = a_p.reshape(bp, tp, N_HEADS * HEAD_DIM) @ w_out_b[lb]
            bs, ts = hs.shape[:2]
            q_s = (hs @ w_q_b[lb]).reshape(bs, ts, N_HEADS, HEAD_DIM)
            a_s = attend_shared(q_s, ext_b[:, :, 0], ext_b[:, :, 1], qpos_s, kpos_b, WIN_B, rel_bias, sinks_b[lb])
            o_s = a_s.reshape(bs, ts, N_HEADS * HEAD_DIM) @ w_out_b[lb]
        xp = xp + o_p
        xs = xs + o_s
        zeros_prev = jnp.zeros((xp.shape[0], CONV_W - 1, 2 * D_FF), xp.dtype)
        f_p, c_p = conv_ffn(rms_norm(xp, norm_ffn[layer]), zeros_prev, w_up[layer], w_conv[layer], b_conv[layer], w_down[layer])
        f_s, c_s = conv_ffn(rms_norm(xs, norm_ffn[layer]), state_conv[layer], w_up[layer], w_conv[layer], b_conv[layer], w_down[layer])
        xp = xp + f_p
        xs = xs + f_s
        conv_p.append(c_p)
        conv_s.append(c_s)
    y_prompt = rms_norm(xp, norm_final)
    y_sample = rms_norm(xs, norm_final)
    kv_a_prompt = jnp.stack(rows_p, axis=2)
    kv_a_sample = jnp.stack(rows_s, axis=2)
    win_a_prompt = jnp.stack(win_p, axis=0)
    win_a_sample = jnp.stack(win_s, axis=0)
    kv_b_prompt = kvb_p[:, t_p - min(WIN_B, t_p):]
    kv_b_sample = ext_b[:, ext_b.shape[1] - wb_b:]
    conv_prompt = jnp.stack(conv_p, axis=0)
    conv_sample = jnp.stack(conv_s, axis=0)
    return (y_prompt, y_sample, kv_a_prompt, win_a_prompt, kv_b_prompt, conv_prompt,
            kv_a_sample, win_a_sample, kv_b_sample, conv_sample)
```

```python
import functools
import math

import jax
import jax.numpy as jnp
from jax import lax
from jax.experimental import pallas as pl
from jax.experimental.pallas import tpu as pltpu

HEAD_DIM = 128
N_KV_A = 4
N_KV_B = 8
L_CMP = 32
CMP_STRIDE = 16
CMP_HID = 2 * HEAD_DIM
SEL_BLOCK = 64
N_SEL = 16
WIN_A = 512
WIN_B = 128
NUM_BUCKETS = 32
MAX_DISTANCE = 2048
CONV_W = 3
RMS_EPS = 1e-6
PAGE_SIZE = 128
Q_BLOCK = 128
SEL_Q_BLOCK = 16

VMEM_LIMIT = 56 * 1024 * 1024
BF16 = jnp.bfloat16
F32 = jnp.float32


def _cparams(sem):
    return pltpu.CompilerParams(dimension_semantics=sem, vmem_limit_bytes=VMEM_LIMIT)


def _norm_kernel(x_ref, g_ref, o_ref):
    x = x_ref[...]
    y = x * lax.rsqrt(jnp.mean(x * x, axis=-1, keepdims=True) + RMS_EPS)
    o_ref[...] = (y * g_ref[...]).astype(o_ref.dtype)


def _norm(x, g, out_dtype=BF16):
    m, d = x.shape
    tm = min(m, 512)
    return pl.pallas_call(
        _norm_kernel,
        grid=(m // tm,),
        in_specs=[pl.BlockSpec((tm, d), lambda i: (i, 0)),
                  pl.BlockSpec((1, d), lambda i: (0, 0))],
        out_specs=pl.BlockSpec((tm, d), lambda i: (i, 0)),
        out_shape=jax.ShapeDtypeStruct((m, d), out_dtype),
        compiler_params=_cparams(("parallel",)),
        name="rmsnorm",
    )(x, g.reshape(1, d))


def _mm_kernel(*refs, has_res, act):
    if has_res:
        x_ref, w_ref, r_ref, o_ref = refs
    else:
        x_ref, w_ref, o_ref = refs
    acc = jnp.dot(x_ref[...], w_ref[...], preferred_element_type=F32)
    if has_res:
        acc = acc + r_ref[...]
    if act == "sigmoid":
        acc = jax.nn.sigmoid(acc)
    o_ref[...] = acc.astype(o_ref.dtype)


def _mm(x, w, *, res=None, act=None, out_dtype=F32, tm=1024, tn=512, x_buffers=None):
    m, k = x.shape
    n = w.shape[1]
    tm = min(tm, m)
    tn = min(tn, n)
    assert m % tm == 0 and n % tn == 0, (m, tm, n, tn)
    x_kwargs = {} if x_buffers is None else dict(pipeline_mode=pl.Buffered(x_buffers))
    in_specs = [pl.BlockSpec((tm, k), lambda i, j: (i, 0), **x_kwargs),
                pl.BlockSpec((k, tn), lambda i, j: (0, j))]
    args = [x, w]
    if res is not None:
        in_specs.append(pl.BlockSpec((tm, tn), lambda i, j: (i, j)))
        args.append(res)
    return pl.pallas_call(
        functools.partial(_mm_kernel, has_res=res is not None, act=act),
        grid=(m // tm, n // tn),
        in_specs=in_specs,
        out_specs=pl.BlockSpec((tm, tn), lambda i, j: (i, j)),
        out_shape=jax.ShapeDtypeStruct((m, n), out_dtype),
        compiler_params=_cparams(("parallel", "arbitrary")),
        name="matmul",
    )(*args)


FFN_TF = 256
CONV_PAD = 8


def _ffn_up_kernel(h_ref, wa_ref, wv_ref, ca_ref, cv_ref, ba_ref, bv_ref, pa_ref, pv_ref,
                   act_ref, sa_ref, sv_ref, buf_ref):
    t = h_ref.shape[0]
    h = h_ref[...]

    def conv(w_ref, c_ref, b_ref, p_ref, s_ref):
        u = jnp.dot(h, w_ref[...], preferred_element_type=F32)
        buf_ref[CONV_PAD - 2:CONV_PAD, :] = p_ref[...]
        buf_ref[CONV_PAD:CONV_PAD + t, :] = u
        s_ref[...] = u[t - 2:t, :]
        c = c_ref[...]
        return (b_ref[...] + c[0:1] * buf_ref[CONV_PAD - 2:CONV_PAD - 2 + t, :]
                + c[1:2] * buf_ref[CONV_PAD - 1:CONV_PAD - 1 + t, :] + c[2:3] * u)

    a = conv(wa_ref, ca_ref, ba_ref, pa_ref, sa_ref)
    v = conv(wv_ref, cv_ref, bv_ref, pv_ref, sv_ref)
    act_ref[...] = (a * jax.nn.sigmoid(a) * v).astype(act_ref.dtype)


def _ffn_up_prompt(h, w_up, w_conv, b_conv, prev, nb, t):
    d = h.shape[1]
    f = w_up.shape[1] // 2
    tf = FFN_TF
    nf = f // tf
    assert f % tf == 0
    b2 = b_conv.reshape(1, 2 * f)
    act, sa, sv = pl.pallas_call(
        _ffn_up_kernel,
        grid=(nb, nf),
        in_specs=[
            pl.BlockSpec((t, d), lambda i, j: (i, 0), pipeline_mode=pl.Buffered(1)),
            pl.BlockSpec((d, tf), lambda i, j: (0, j)),
            pl.BlockSpec((d, tf), lambda i, j: (0, nf + j)),
            pl.BlockSpec((CONV_W, tf), lambda i, j: (0, j)),
            pl.BlockSpec((CONV_W, tf), lambda i, j: (0, nf + j)),
            pl.BlockSpec((1, tf), lambda i, j: (0, j)),
            pl.BlockSpec((1, tf), lambda i, j: (0, nf + j)),
            pl.BlockSpec((None, 2, tf), lambda i, j: (i, 0, j)),
            pl.BlockSpec((None, 2, tf), lambda i, j: (i, 0, nf + j)),
        ],
        out_specs=[
            pl.BlockSpec((t, tf), lambda i, j: (i, j)),
            pl.BlockSpec((None, 2, tf), lambda i, j: (i, 0, j)),
            pl.BlockSpec((None, 2, tf), lambda i, j: (i, 0, j)),
        ],
        out_shape=[
            jax.ShapeDtypeStruct((nb * t, f), BF16),
            jax.ShapeDtypeStruct((nb, 2, f), F32),
            jax.ShapeDtypeStruct((nb, 2, f), F32),
        ],
        scratch_shapes=[pltpu.VMEM((CONV_PAD + t, tf), F32)],
        compiler_params=_cparams(("parallel", "arbitrary")),
        name="ffn_up_prompt",
    )(h, w_up, w_up, w_conv, w_conv, b2, b2, prev, prev)
    return act, jnp.concatenate([sa, sv], axis=-1)


def _ffn_up_sample_kernel(h_ref, wa_ref, wv_ref, ca_ref, cv_ref, ba_ref, bv_ref, pa_ref, pv_ref,
                          act_ref, ua_ref, uv_ref):
    h = h_ref[...]

    def conv(w_ref, c_ref, b_ref, p_ref, u_ref):
        u = jnp.dot(h, w_ref[...], preferred_element_type=F32)
        u_ref[...] = u
        c = c_ref[...]
        return b_ref[...] + c[0:1] * p_ref[0] + c[1:2] * p_ref[1] + c[2:3] * u

    a = conv(wa_ref, ca_ref, ba_ref, pa_ref, ua_ref)
    v = conv(wv_ref, cv_ref, bv_ref, pv_ref, uv_ref)
    act_ref[...] = (a * jax.nn.sigmoid(a) * v).astype(act_ref.dtype)


def _ffn_up_sample(h, w_up, w_conv, b_conv, prev_t):
    nb, d = h.shape
    f = w_up.shape[1] // 2
    tf = FFN_TF
    nf = f // tf
    b2 = b_conv.reshape(1, 2 * f)
    act, ua, uv = pl.pallas_call(
        _ffn_up_sample_kernel,
        grid=(nf,),
        in_specs=[
            pl.BlockSpec((nb, d), lambda j: (0, 0)),
            pl.BlockSpec((d, tf), lambda j: (0, j)),
            pl.BlockSpec((d, tf), lambda j: (0, nf + j)),
            pl.BlockSpec((CONV_W, tf), lambda j: (0, j)),
            pl.BlockSpec((CONV_W, tf), lambda j: (0, nf + j)),
            pl.BlockSpec((1, tf), lambda j: (0, j)),
            pl.BlockSpec((1, tf), lambda j: (0, nf + j)),
            pl.BlockSpec((2, nb, tf), lambda j: (0, 0, j)),
            pl.BlockSpec((2, nb, tf), lambda j: (0, 0, nf + j)),
        ],
        out_specs=[
            pl.BlockSpec((nb, tf), lambda j: (0, j)),
            pl.BlockSpec((nb, tf), lambda j: (0, j)),
            pl.BlockSpec((nb, tf), lambda j: (0, j)),
        ],
        out_shape=[
            jax.ShapeDtypeStruct((nb, f), BF16),
            jax.ShapeDtypeStruct((nb, f), F32),
            jax.ShapeDtypeStruct((nb, f), F32),
        ],
        compiler_params=_cparams(("arbitrary",)),
        name="ffn_up_sample",
    )(h, w_up, w_up, w_conv, w_conv, b2, b2, prev_t, prev_t)
    return act, jnp.concatenate([ua, uv], axis=-1)


def rel_bucket(rel):
    n = jnp.maximum(rel, 0)
    max_exact = NUM_BUCKETS // 2
    nf = jnp.maximum(n, 1).astype(jnp.float32)
    large = max_exact + (jnp.log(nf / max_exact) / math.log(MAX_DISTANCE / max_exact)
                         * (NUM_BUCKETS - max_exact)).astype(jnp.int32)
    large = jnp.minimum(large, NUM_BUCKETS - 1)
    return jnp.where(n < max_exact, n, large)


def head_bias(rel, rel_bias, n_groups):
    tq, s = rel.shape
    n_heads = rel_bias.shape[1]
    b = rel_bias.astype(jnp.float32)[rel_bucket(rel)]
    return b.reshape(tq, s, n_groups, n_heads // n_groups).transpose(2, 3, 0, 1)


def masked_softmax(logits, mask, sink=None):
    neg = jnp.finfo(jnp.float32).min
    l = jnp.where(mask, logits, neg)
    m = jnp.max(l, axis=-1, keepdims=True)
    if sink is not None:
        m = jnp.maximum(m, sink)
    e = jnp.where(mask, jnp.exp(l - m), 0.0)
    den = jnp.sum(e, axis=-1, keepdims=True)
    if sink is not None:
        den = den + jnp.exp(sink - m)
    return e / jnp.maximum(den, 1e-30)


def attend_shared(q, k, v, qpos, kpos, window, rel_bias, sink=None):
    b, tq, h, dh = q.shape
    g = k.shape[2]
    qg = q.reshape(b, tq, g, h // g, dh)
    s = jnp.einsum('btgrd,bsgd->bgrts', qg, k).astype(jnp.float32) * (dh ** -0.5)
    rel = qpos[:, None] - kpos[None, :]
    mask = (rel >= 0) & (rel <= window) & (kpos >= 0)[None, :]
    s = s + head_bias(rel, rel_bias, g)
    sk = None if sink is None else sink.astype(jnp.float32).reshape(g, h // g, 1, 1)
    p = masked_softmax(s, mask, sk)
    o = jnp.einsum('bgrts,bsgd->btgrd', p.astype(v.dtype), v)
    return o.reshape(b, tq, h, dh)


def window_attn_prompt(q, k, v, window, rel_bias, sink=None):
    b, t, h, dh = q.shape
    nq = t // Q_BLOCK
    span = Q_BLOCK + window
    kp = jnp.pad(k, ((0, 0), (window, 0), (0, 0), (0, 0)))
    vp = jnp.pad(v, ((0, 0), (window, 0), (0, 0), (0, 0)))
    qb = jnp.moveaxis(q.reshape(b, nq, Q_BLOCK, h, dh), 1, 0)

    def blk(args):
        i, qi = args
        start = i * Q_BLOCK
        ki = lax.dynamic_slice_in_dim(kp, start, span, axis=1)
        vi = lax.dynamic_slice_in_dim(vp, start, span, axis=1)
        qpos = start + jnp.arange(Q_BLOCK)
        kpos = start - window + jnp.arange(span)
        return attend_shared(qi, ki, vi, qpos, kpos, window, rel_bias, sink)

    out = lax.map(blk, (jnp.arange(nq), qb))
    return jnp.moveaxis(out, 0, 1).reshape(b, t, h, dh)


def half_block_proj(rows, w1):
    b, l, c, g, dh = rows.shape
    nh = l // CMP_STRIDE
    half = rows[:, :nh * CMP_STRIDE].reshape(b, nh, CMP_STRIDE, c, g, dh)
    w1r = w1.reshape(2, 2, CMP_STRIDE, dh, CMP_HID)
    return jnp.einsum('bnscgd,cesdh->ebncgh', half, w1r)


def compress_from_halves(hb, w1, w2, pe):
    pe_term = jnp.einsum('cld,cldh->ch', pe, w1.reshape(2, L_CMP, HEAD_DIM, CMP_HID))
    h = jax.nn.gelu(hb[0, :, :-1] + hb[1, :, 1:] + pe_term[:, None, :])
    return jnp.einsum('bncgh,chd->bncgd', h, w2)


def nsa_cmp_sel(q, qpos, kc, vc, nsel, fetch, rel_bias):
    b, tq, h, dh = q.shape
    g = kc.shape[2]
    r = h // g
    nc = kc.shape[1]
    scale = dh ** -0.5
    qg = q.reshape(b, tq, g, r, dh)
    cend = CMP_STRIDE * jnp.arange(nc) + (L_CMP - 1)
    rel_c = qpos[:, None] - cend[None, :]
    s_c = jnp.einsum('btgrd,bcgd->bgrtc', qg, kc).astype(jnp.float32) * scale + head_bias(rel_c, rel_bias, g)
    p_c = masked_softmax(s_c, rel_c >= 0)
    o_c = jnp.einsum('bgrtc,bcgd->btgrd', p_c.astype(vc.dtype), vc).reshape(b, tq, h, dh)
    cst = CMP_STRIDE * jnp.arange(nc)[:, None]
    sst = SEL_BLOCK * jnp.arange(nsel)[None, :]
    ov = jnp.maximum(jnp.minimum(cst + L_CMP, sst + SEL_BLOCK) - jnp.maximum(cst, sst), 0).astype(jnp.float32) / L_CMP
    imp = jnp.einsum('bgrtc,cj->bgtj', p_c, ov)
    jblk = jnp.arange(nsel)[None, :]
    cur = (qpos // SEL_BLOCK)[:, None]
    forced = (jblk == 0) | (jblk == cur) | (jblk == cur - 1)
    imp = jnp.where(forced, jnp.inf, jnp.where(jblk <= cur, imp, -jnp.inf))
    k_top = min(N_SEL, nsel)
    _, idx = lax.top_k(imp, k_top)
    kv = fetch(idx)
    s_len = k_top * SEL_BLOCK
    ks = kv[..., 0, :].reshape(b, g, tq, s_len, dh)
    vs = kv[..., 1, :].reshape(b, g, tq, s_len, dh)
    kpos = (idx[..., None] * SEL_BLOCK + jnp.arange(SEL_BLOCK)).reshape(b, g, tq, s_len)
    rel_s = qpos[None, None, :, None] - kpos
    tab = rel_bias.astype(jnp.float32).reshape(NUM_BUCKETS, g, r)
    bias_s = tab[rel_bucket(rel_s), jnp.arange(g)[None, :, None, None]]
    s_s = jnp.einsum('btgrd,bgtsd->bgrts', qg, ks).astype(jnp.float32) * scale + jnp.moveaxis(bias_s, -1, 2)
    p_s = masked_softmax(s_s, (rel_s >= 0)[:, :, None])
    o_s = jnp.einsum('bgrts,bgtsd->btgrd', p_s.astype(vs.dtype), vs).reshape(b, tq, h, dh)
    return o_c, o_s


def _nsa_core_prompt(q, kv, rel_bias, w1, w2, pe):
    b, t = q.shape[:2]
    n_heads = q.shape[2]
    comp = compress_from_halves(half_block_proj(kv[:, :, 0], w1), w1, w2, pe)
    kc, vc = comp[:, :, 0], comp[:, :, 1]
    blocks = kv[:, :, 1].reshape(b, t // SEL_BLOCK, SEL_BLOCK, 2, N_KV_A, HEAD_DIM)
    b6 = jnp.arange(b)[:, None, None, None, None, None]
    g6 = jnp.arange(N_KV_A)[None, :, None, None, None, None]
    sb = jnp.arange(SEL_BLOCK)[:, None]
    kvi = jnp.arange(2)

    def fetch(idx):
        return blocks[b6, idx[..., None, None], sb, kvi, g6]

    nsel = t // SEL_BLOCK
    nqb = t // SEL_Q_BLOCK
    qb = jnp.moveaxis(q.reshape(b, nqb, SEL_Q_BLOCK, n_heads, HEAD_DIM), 1, 0)

    def blk(args):
        i, qi = args
        qpos = i * SEL_Q_BLOCK + jnp.arange(SEL_Q_BLOCK)
        return nsa_cmp_sel(qi, qpos, kc, vc, nsel, fetch, rel_bias)

    o_c, o_s = lax.map(blk, (jnp.arange(nqb), qb))
    o_c = jnp.moveaxis(o_c, 0, 1).reshape(b, t, n_heads, HEAD_DIM)
    o_s = jnp.moveaxis(o_s, 0, 1).reshape(b, t, n_heads, HEAD_DIM)
    o_w = window_attn_prompt(q, kv[:, :, 2, 0], kv[:, :, 2, 1], WIN_A, rel_bias)
    return o_c, o_s, o_w


def _nsa_core_sample(q, kv, layer, cache_kv_a, win_buf, page_table, rel_bias, w1, w2, pe):
    b, t = q.shape[:2]
    n_a_layers = cache_kv_a.shape[2]
    n_pages = page_table.shape[1]
    past = n_pages * PAGE_SIZE
    past_cmp = cache_kv_a[page_table[:, :, None, None], jnp.arange(PAGE_SIZE)[None, None, :, None],
                          layer, jnp.arange(2)[None, None, None, :]]
    past_cmp = past_cmp.reshape(b, past, 2, N_KV_A, HEAD_DIM)
    hb = jnp.concatenate([half_block_proj(past_cmp, w1), half_block_proj(kv[:, :, 0], w1)], axis=2)
    comp = compress_from_halves(hb, w1, w2, pe)
    kc, vc = comp[:, :, 0], comp[:, :, 1]
    npb = past // SEL_BLOCK
    r_pg = PAGE_SIZE // SEL_BLOCK
    pool_r = cache_kv_a.reshape(cache_kv_a.shape[0], r_pg, SEL_BLOCK, n_a_layers, 4, N_KV_A, HEAD_DIM)
    n_tail = -(-t // SEL_BLOCK)
    tail = jnp.pad(kv[:, :, 1], ((0, 0), (0, n_tail * SEL_BLOCK - t), (0, 0), (0, 0), (0, 0)))
    tail = tail.reshape(b, n_tail, SEL_BLOCK, 2, N_KV_A, HEAD_DIM)
    b6 = jnp.arange(b)[:, None, None, None, None, None]
    g6 = jnp.arange(N_KV_A)[None, :, None, None, None, None]
    sb = jnp.arange(SEL_BLOCK)[:, None]
    kvi = jnp.arange(2)

    def fetch(idx):
        i6 = idx[..., None, None]
        jp = jnp.minimum(i6, npb - 1)
        phys = page_table[b6, jp // r_pg]
        from_pool = pool_r[phys, jp % r_pg, sb, layer, kvi + 2, g6]
        jt = jnp.clip(i6 - npb, 0, n_tail - 1)
        from_tail = tail[b6, jt, sb, kvi, g6]
        return jnp.where(i6[..., None] >= npb, from_tail, from_pool)

    nsel = -(-(past + t) // SEL_BLOCK)
    qpos = past + jnp.arange(t)
    o_c, o_s = nsa_cmp_sel(q, qpos, kc, vc, nsel, fetch, rel_bias)
    wb = win_buf.shape[1]
    ext = jnp.concatenate([win_buf.astype(kv.dtype), kv[:, :, 2]], axis=1)
    kpos = past - wb + jnp.arange(wb + t)
    o_w = attend_shared(q, ext[:, :, 0], ext[:, :, 1], qpos, kpos, WIN_A, rel_bias)
    return o_c, o_s, o_w, ext[:, t:]


def kernel(x_prompt, x_sample, cache_kv_a, state_win_a, state_kv_b, state_conv, page_table, rel_bias, norm_attn,
           norm_ffn, norm_kv_b, norm_final, w_in_a, w_cmp1, w_cmp2, pe_cmp, w_out_a, w_kv_b, w_q_b, sinks_b,
           w_out_b, w_up, w_conv, b_conv, w_down):
    bp, tp, d = x_prompt.shape
    bs, ts, _ = x_sample.shape
    depth = norm_attn.shape[0]
    n_a = w_in_a.shape[0]
    n_heads = rel_bias.shape[1]
    qd = n_heads * HEAD_DIM
    nkv = 6 * N_KV_A * HEAD_DIM
    past_len = page_table.shape[1] * PAGE_SIZE
    wb_b = state_kv_b.shape[1]
    mp = bp * tp
    ms = bs * ts

    xp = x_prompt.reshape(mp, d)
    xs = x_sample.reshape(ms, d)
    zeros_prev = jnp.zeros((bp, CONV_W - 1, w_up.shape[2]), F32)
    qpos_s = past_len + jnp.arange(ts)

    rows_p, rows_s, win_p, win_s, conv_p, conv_s = [], [], [], [], [], []
    for layer in range(depth):
        hp = _norm(xp, norm_attn[layer])
        hs = _norm(xs, norm_attn[layer])
        if layer < n_a:
            w_qkv = w_in_a[layer, :, :qd + nkv].astype(BF16)
            w_gate = jnp.pad(w_in_a[layer, :, qd + nkv:], ((0, 0), (0, 128 - 3 * n_heads))).astype(BF16)
            w_out = w_out_a[layer].astype(BF16)
            z = _mm(hp, w_qkv)
            gates = _mm(hp, w_gate, act="sigmoid")[:, :3 * n_heads].reshape(bp, tp, n_heads, 3)
            q = z[:, :qd].reshape(bp, tp, n_heads, HEAD_DIM)
            kv = z[:, qd:].reshape(bp, tp, 3, 2, N_KV_A, HEAD_DIM)
            o_c, o_s, o_w = _nsa_core_prompt(q, kv, rel_bias, w_cmp1[layer], w_cmp2[layer], pe_cmp[layer])
            o = o_c * gates[..., 0:1] + o_s * gates[..., 1:2] + o_w * gates[..., 2:3]
            xp = _mm(o.reshape(mp, qd).astype(BF16), w_out, res=xp)
            rows_p.append(kv[:, :, :2].reshape(bp, tp, 4, N_KV_A, HEAD_DIM))
            win_p.append(kv[:, tp - min(WIN_A, tp):, 2])
            z = _mm(hs, w_qkv)
            gates = _mm(hs, w_gate, act="sigmoid")[:, :3 * n_heads].reshape(bs, ts, n_heads, 3)
            q = z[:, :qd].reshape(bs, ts, n_heads, HEAD_DIM)
            kv = z[:, qd:].reshape(bs, ts, 3, 2, N_KV_A, HEAD_DIM)
            o_c, o_s, o_w, buf = _nsa_core_sample(q, kv, layer, cache_kv_a, state_win_a[layer], page_table,
                                                  rel_bias, w_cmp1[layer], w_cmp2[layer], pe_cmp[layer])
            o = o_c * gates[..., 0:1] + o_s * gates[..., 1:2] + o_w * gates[..., 2:3]
            xs = _mm(o.reshape(ms, qd).astype(BF16), w_out, res=xs)
            rows_s.append(kv[:, :, :2].reshape(bs, ts, 4, N_KV_A, HEAD_DIM))
            win_s.append(buf)
        else:
            if layer == n_a:
                w_kv = w_kv_b.astype(BF16)
                kvb_p = _mm(_norm(xp, norm_kv_b), w_kv).reshape(bp, tp, 2, N_KV_B, HEAD_DIM)
                kvb_s = _mm(_norm(xs, norm_kv_b), w_kv).reshape(bs, ts, 2, N_KV_B, HEAD_DIM)
                ext_b = jnp.concatenate([state_kv_b, kvb_s], axis=1)
                kpos_b = past_len - wb_b + jnp.arange(ext_b.shape[1])
            lb = layer - n_a
            w_q = w_q_b[lb].astype(BF16)
            w_out = w_out_b[lb].astype(BF16)
            q_p = _mm(hp, w_q).reshape(bp, tp, n_heads, HEAD_DIM)
            a_p = window_attn_prompt(q_p, kvb_p[:, :, 0], kvb_p[:, :, 1], WIN_B, rel_bias, sinks_b[lb])
            xp = _mm(a_p.reshape(mp, qd).astype(BF16), w_out, res=xp)
            q_s = _mm(hs, w_q).reshape(bs, ts, n_heads, HEAD_DIM)
            a_s = attend_shared(q_s, ext_b[:, :, 0], ext_b[:, :, 1], qpos_s, kpos_b, WIN_B, rel_bias, sinks_b[lb])
            xs = _mm(a_s.reshape(ms, qd).astype(BF16), w_out, res=xs)
        w_up_l = w_up[layer].astype(BF16)
        w_down_l = w_down[layer].astype(BF16)
        act_p, c_p = _ffn_up_prompt(_norm(xp, norm_ffn[layer]), w_up_l, w_conv[layer], b_conv[layer], zeros_prev,
                                    bp, tp)
        xp = _mm(act_p, w_down_l, res=xp, tm=1024, tn=256, x_buffers=1)
        prev_t = jnp.swapaxes(state_conv[layer], 0, 1)
        act_s, u_s = _ffn_up_sample(_norm(xs, norm_ffn[layer]), w_up_l, w_conv[layer], b_conv[layer], prev_t)
        xs = _mm(act_s, w_down_l, res=xs, tn=256)
        conv_p.append(c_p)
        conv_s.append(jnp.concatenate([state_conv[layer][:, 1:], u_s[:, None, :]], axis=1))

    y_prompt = _norm(xp, norm_final, out_dtype=F32).reshape(bp, tp, d)
    y_sample = _norm(xs, norm_final, out_dtype=F32).reshape(bs, ts, d)
    kv_a_prompt = jnp.stack(rows_p, axis=2)
    kv_a_sample = jnp.stack(rows_s, axis=2)
    win_a_prompt = jnp.stack(win_p, axis=0)
    win_a_sample = jnp.stack(win_s, axis=0)
    kv_b_prompt = kvb_p[:, tp - min(WIN_B, tp):]
    kv_b_sample = ext_b[:, ext_b.shape[1] - wb_b:]
    conv_prompt = jnp.stack(conv_p, axis=0)
    conv_sample = jnp.stack(conv_s, axis=0)
    return (y_prompt, y_sample, kv_a_prompt, win_a_prompt, kv_b_prompt, conv_prompt,
            kv_a_sample, win_a_sample, kv_b_sample, conv_sample)
```
